```python
import jax, jax.numpy as jnp
from jax import lax
import numpy as np

D_MODEL = 2048
BATCH = 8
SEQ = 4096
DEPTH = 4

N_MIXERS = 3
SC_WIDTH = 3
POOL_WINDOWS = (2, 4, 8, 16)
N_POOL_GROUPS = 4
POOL_GROUP_DIM = D_MODEL // N_POOL_GROUPS
CF_WIDTH = 31
D_FF = 5632
N_EXPERTS = 8
TOP_K = 2
D_FF_EXPERT = 5632
RMS_EPS = 1e-6
LN_EPS = 1e-5
ADA_INIT = 0.5

N_SC = (DEPTH + 2) // 3
N_POOL = (DEPTH + 1) // 3
N_CF = DEPTH // 3
N_DENSE = (DEPTH + 1) // 2
N_MOE = DEPTH // 2

kernel_name = "hybrid_conv_pool_conformer_moe_trunk"


def rms_norm(x, g):
    xf = x.astype(jnp.float32)
    y = xf * lax.rsqrt(jnp.mean(xf * xf, axis=-1, keepdims=True) + RMS_EPS)
    return (y * g.astype(jnp.float32)).astype(x.dtype)


def layer_norm(x, g, b):
    xf = x.astype(jnp.float32)
    mu = jnp.mean(xf, axis=-1, keepdims=True)
    var = jnp.mean(jnp.square(xf - mu), axis=-1, keepdims=True)
    y = (xf - mu) * lax.rsqrt(var + LN_EPS)
    return (y * g.astype(jnp.float32) + b.astype(jnp.float32)).astype(x.dtype)


def causal_depthwise_conv(x, w):
    k = w.shape[0]
    return lax.conv_general_dilated(
        x, w[:, None, :], window_strides=(1,), padding=((k - 1, 0),),
        dimension_numbers=("NWC", "WIO", "NWC"), feature_group_count=x.shape[-1])


def short_conv_mixer(h, w_in, conv_w, w_out):
    b_gate, c_gate, v = jnp.split(h @ w_in, 3, axis=-1)
    u = causal_depthwise_conv(c_gate * v, conv_w)
    return (b_gate * u) @ w_out


def pooling_mixer(h, w_grp, ch_scale):
    b, s, d = h.shape
    hg = h.reshape(b, s, N_POOL_GROUPS, POOL_GROUP_DIM).astype(jnp.float32)
    cs = jnp.concatenate(
        [jnp.zeros((b, 1, N_POOL_GROUPS, POOL_GROUP_DIM), jnp.float32),
         jnp.cumsum(hg, axis=1)], axis=1)
    t = jnp.arange(s)
    pooled = []
    for gi, win in enumerate(POOL_WINDOWS):
        start = jnp.maximum(t + 1 - win, 0)
        count = (t + 1 - start).astype(jnp.float32)
        csg = cs[:, :, gi]
        pooled.append((csg[:, 1:] - csg[:, start]) / count[None, :, None])
    pooled = jnp.stack(pooled, axis=2) - hg
    y = jnp.einsum("bsgc,gcd->bsgd", pooled.astype(h.dtype), w_grp)
    return y.reshape(b, s, d) * ch_scale


def conformer_conv_mixer(h, w_pw1, b_pw1, w_dw, b_dw, ln_g, ln_b, w_pw2, b_pw2):
    a, gate = jnp.split(h @ w_pw1 + b_pw1, 2, axis=-1)
    u = a * jax.nn.sigmoid(gate)
    u = causal_depthwise_conv(u, w_dw) + b_dw
    u = jax.nn.silu(layer_norm(u, ln_g, ln_b))
    return u @ w_pw2 + b_pw2


def swiglu(h, w_gate, w_up, w_down):
    return (jax.nn.silu(h @ w_gate) * (h @ w_up)) @ w_down


def moe_swiglu(h, w_router, w_gate, w_up, w_down):
    b, s, d = h.shape
    ht = h.reshape(b * s, d)
    logits = (ht @ w_router).astype(jnp.float32)
    top_vals, top_idx = lax.top_k(logits, TOP_K)
    top_w = jax.nn.softmax(top_vals, axis=-1)
    gates = jnp.sum(jax.nn.one_hot(top_idx, N_EXPERTS, dtype=jnp.float32)
                    * top_w[..., None], axis=1)
    out = jnp.zeros((b * s, d), jnp.float32)
    for e in range(N_EXPERTS):
        ye = swiglu(ht, w_gate[e], w_up[e], w_down[e]).astype(jnp.float32)
        out = out + gates[:, e:e + 1] * ye
    return out.astype(h.dtype).reshape(b, s, d)


def setup_inputs(seed: int = 0) -> dict:
    key = jax.random.key(seed)
    ks = iter(jax.random.split(key, 32))

    def nrm(shape, scale):
        return jax.random.normal(next(ks), shape, jnp.float32) * scale

    D = D_MODEL
    fi = D ** -0.5
    return {
        "x": nrm((BATCH, SEQ, D), 1.0),
        "c": nrm((BATCH, D), 1.0),
        "ada_w": nrm((DEPTH, D, 6 * D), ADA_INIT * fi),
        "ada_b": nrm((DEPTH, 6 * D), 0.02),
        "norm_g": 1.0 + nrm((DEPTH, 4, D), 0.05),
        "sc_w_in": nrm((N_SC, D, 3 * D), fi),
        "sc_conv": nrm((N_SC, SC_WIDTH, D), SC_WIDTH ** -0.5),
        "sc_w_out": nrm((N_SC, D, D), fi),
        "pool_w_grp": nrm((N_POOL, N_POOL_GROUPS, POOL_GROUP_DIM, POOL_GROUP_DIM), POOL_GROUP_DIM ** -0.5),
        "pool_scale": 1.0 + nrm((N_POOL, D), 0.1),
        "cf_w_pw1": nrm((N_CF, D, 2 * D), fi),
        "cf_b_pw1": nrm((N_CF, 2 * D), 0.02),
        "cf_w_dw": nrm((N_CF, CF_WIDTH, D), CF_WIDTH ** -0.5),
        "cf_b_dw": nrm((N_CF, D), 0.02),
        "cf_ln_g": 1.0 + nrm((N_CF, D), 0.05),
        "cf_ln_b": nrm((N_CF, D), 0.02),
        "cf_w_pw2": nrm((N_CF, D, D), fi),
        "cf_b_pw2": nrm((N_CF, D), 0.02),
        "ffn_w_gate": nrm((N_DENSE, D, D_FF), fi),
        "ffn_w_up": nrm((N_DENSE, D, D_FF), fi),
        "ffn_w_down": nrm((N_DENSE, D_FF, D), D_FF ** -0.5),
        "moe_router": nrm((N_MOE, D, N_EXPERTS), fi),
        "moe_w_gate": nrm((N_MOE, N_EXPERTS, D, D_FF_EXPERT), fi),
        "moe_w_up": nrm((N_MOE, N_EXPERTS, D, D_FF_EXPERT), fi),
        "moe_w_down": nrm((N_MOE, N_EXPERTS, D_FF_EXPERT, D), D_FF_EXPERT ** -0.5),
    }


def reference(x, c, ada_w, ada_b, norm_g, sc_w_in, sc_conv, sc_w_out,
              pool_w_grp, pool_scale, cf_w_pw1, cf_b_pw1, cf_w_dw, cf_b_dw,
              cf_ln_g, cf_ln_b, cf_w_pw2, cf_b_pw2, ffn_w_gate, ffn_w_up,
              ffn_w_down, moe_router, moe_w_gate, moe_w_up, moe_w_down):
    cond = jax.nn.silu(c)
    h = x
    for i in range(DEPTH):
        mod = cond @ ada_w[i] + ada_b[i]
        sh1, sc1, g1, sh2, sc2, g2 = [m[:, None, :] for m in jnp.split(mod, 6, axis=-1)]

        u = rms_norm(h, norm_g[i, 0]) * (1.0 + sc1) + sh1
        kind = i % N_MIXERS
        j = i // N_MIXERS
        if kind == 0:
            y = short_conv_mixer(u, sc_w_in[j], sc_conv[j], sc_w_out[j])
        elif kind == 1:
            y = pooling_mixer(u, pool_w_grp[j], pool_scale[j])
        else:
            y = conformer_conv_mixer(u, cf_w_pw1[j], cf_b_pw1[j], cf_w_dw[j], cf_b_dw[j],
                                     cf_ln_g[j], cf_ln_b[j], cf_w_pw2[j], cf_b_pw2[j])
        h = h + g1 * rms_norm(y, norm_g[i, 1])

        u = rms_norm(h, norm_g[i, 2]) * (1.0 + sc2) + sh2
        f = i // 2
        if i % 2 == 0:
            y = swiglu(u, ffn_w_gate[f], ffn_w_up[f], ffn_w_down[f])
        else:
            y = moe_swiglu(u, moe_router[f], moe_w_gate[f], moe_w_up[f], moe_w_down[f])
        h = h + g2 * rms_norm(y, norm_g[i, 3])
    return h
```

```python
import functools

import jax
import jax.numpy as jnp
from jax import lax
from jax.experimental import pallas as pl
from jax.experimental.pallas import tpu as pltpu

RMS_EPS = 1e-6
LN_EPS = 1e-5
POOL_WINDOWS = (2, 4, 8, 16)
N_MIXERS = 3
ROUTE_COLS = 8

VMEM_LIMIT_BYTES = 56 * 1024 * 1024
SUBLANES = 8
BF16 = jnp.bfloat16
F32 = jnp.float32


def _pick(n, pref):
    t = min(n, pref)
    while n % t:
        t //= 2
    return t


def _dot(a, b):
    return jnp.dot(a, b, preferred_element_type=F32)


def _silu(x):
    return x / (1.0 + jnp.exp(-x))


def _rms(x, g):
    ms = jnp.mean(x * x, axis=-1, keepdims=True)
    return x * lax.rsqrt(ms + RMS_EPS) * g


def _prenorm(h, g, shift, scale):
    return _rms(h, g) * (1.0 + scale) + shift


def _load_halo(ext_scr, carry_ref, seq_start):
    halo = carry_ref.shape[0]

    @pl.when(seq_start)
    def _():
        ext_scr[0:halo, :] = jnp.zeros((halo, ext_scr.shape[-1]), ext_scr.dtype)

    @pl.when(jnp.logical_not(seq_start))
    def _():
        ext_scr[0:halo, :] = carry_ref[...]


def _params(sem):
    return pltpu.CompilerParams(dimension_semantics=sem, vmem_limit_bytes=VMEM_LIMIT_BYTES)


def _ada_kernel(c_ref, w_ref, b_ref, o_ref):
    cond = _silu(c_ref[...]).astype(BF16)
    o_ref[...] = _dot(cond, w_ref[...].astype(BF16)) + b_ref[...]


def _ada(c, ada_w, ada_b):
    depth, d, n6 = ada_w.shape
    b = c.shape[0]
    tn = _pick(n6, 1024)
    return pl.pallas_call(
        _ada_kernel,
        out_shape=jax.ShapeDtypeStruct((depth, b, n6), F32),
        grid=(depth, n6 // tn),
        in_specs=[
            pl.BlockSpec((b, d), lambda l, n: (0, 0)),
            pl.BlockSpec((None, d, tn), lambda l, n: (l, 0, n)),
            pl.BlockSpec((None, 1, tn), lambda l, n: (l, 0, n)),
        ],
        out_specs=pl.BlockSpec((None, b, tn), lambda l, n: (l, 0, n)),
        compiler_params=_params(("arbitrary", "arbitrary")),
        name="ada_mod",
    )(c, ada_w, ada_b.reshape(depth, 1, n6))


def _cond_specs(layer, tm, seq, d, ngrid):
    if ngrid == 1:
        mod = pl.BlockSpec((None, None, 6, d), lambda m: (layer, (m * tm) // seq, 0, 0))
        ng = pl.BlockSpec((None, 4, d), lambda m: (layer, 0, 0))
    else:
        mod = pl.BlockSpec((None, None, 6, d), lambda m, c: (layer, (m * tm) // seq, 0, 0))
        ng = pl.BlockSpec((None, 4, d), lambda m, c: (layer, 0, 0))
    return mod, ng


def _sconv_kernel(h_ref, mod_ref, ng_ref, wb_ref, wc_ref, wv_ref, cw_ref, wo_ref, o_ref,
                  u_scr, ext_scr, carry_scr, *, tm, seq):
    m = pl.program_id(0)
    c = pl.program_id(1)

    @pl.when(c == 0)
    def _():
        u = _prenorm(h_ref[...], ng_ref[0:1, :], mod_ref[0:1, :], mod_ref[1:2, :])
        u_scr[...] = u.astype(BF16)
        o_ref[...] = jnp.zeros_like(o_ref)

    u = u_scr[...]
    cv = _dot(u, wc_ref[...]) * _dot(u, wv_ref[...])
    _load_halo(ext_scr, carry_scr.at[c], (m * tm) % seq == 0)
    ext_scr[SUBLANES:SUBLANES + tm, :] = cv
    carry_scr[c] = cv[tm - SUBLANES:, :]
    cw = cw_ref[...]
    conv = (cw[2:3, :] * cv
            + cw[1:2, :] * ext_scr[pl.ds(SUBLANES - 1, tm), :]
            + cw[0:1, :] * ext_scr[pl.ds(SUBLANES - 2, tm), :])
    z = (_dot(u, wb_ref[...]) * conv).astype(BF16)
    o_ref[...] += _dot(z, wo_ref[...])

    @pl.when(c == pl.num_programs(1) - 1)
    def _():
        o_ref[...] = h_ref[...] + mod_ref[2:3, :] * _rms(o_ref[...], ng_ref[1:2, :])


def _sconv_layer(h, mod, norm_g, layer, seq, w_in, conv_w, w_out):
    n, d = h.shape
    tm = _pick(seq, 512)
    tc = _pick(d, 512)
    nc = d // tc
    mod_spec, ng_spec = _cond_specs(layer, tm, seq, d, 2)
    kern = functools.partial(_sconv_kernel, tm=tm, seq=seq)
    return pl.pallas_call(
        kern,
        out_shape=jax.ShapeDtypeStruct((n, d), F32),
        grid=(n // tm, nc),
        in_specs=[
            pl.BlockSpec((tm, d), lambda m, c: (m, 0), pipeline_mode=pl.Buffered(1)),
            mod_spec, ng_spec,
            pl.BlockSpec((d, tc), lambda m, c: (0, c)),
            pl.BlockSpec((d, tc), lambda m, c: (0, nc + c)),
            pl.BlockSpec((d, tc), lambda m, c: (0, 2 * nc + c)),
            pl.BlockSpec((conv_w.shape[0], tc), lambda m, c: (0, c)),
            pl.BlockSpec((tc, d), lambda m, c: (c, 0)),
        ],
        out_specs=pl.BlockSpec((tm, d), lambda m, c: (m, 0)),
        scratch_shapes=[
            pltpu.VMEM((tm, d), BF16),
            pltpu.VMEM((tm + SUBLANES, tc), F32),
            pltpu.VMEM((nc, SUBLANES, tc), F32),
        ],
        compiler_params=_params(("arbitrary", "arbitrary")),
        name="sconv_mixer",
    )(h, mod, norm_g, w_in, w_in, w_in, conv_w, w_out)


POOL_HALO = 16


def _pool_kernel(h_ref, mod_ref, ng_ref, wg_ref, cs_ref, o_ref, ext_scr, carry_scr, *, tm, seq):
    m = pl.program_id(0)
    h = h_ref[...]
    d = h.shape[-1]
    u = _prenorm(h, ng_ref[0:1, :], mod_ref[0:1, :], mod_ref[1:2, :])
    t0 = (m * tm) % seq
    _load_halo(ext_scr, carry_scr, t0 == 0)
    ext_scr[POOL_HALO:POOL_HALO + tm, :] = u
    carry_scr[...] = u[tm - POOL_HALO:, :]
    t_glob = t0 + lax.broadcasted_iota(jnp.int32, (tm, 1), 0)
    ngroups = wg_ref.shape[0]
    gd = d // ngroups
    ys = []
    for g in range(ngroups):
        win = POOL_WINDOWS[g]
        lo, hi = g * gd, (g + 1) * gd
        s = u[:, lo:hi]
        for j in range(1, win):
            s = s + ext_scr[pl.ds(POOL_HALO - j, tm), lo:hi]
        cnt = jnp.minimum(t_glob + 1, win).astype(F32)
        pooled = s / cnt - u[:, lo:hi]
        ys.append(_dot(pooled.astype(BF16), wg_ref[g]))
    y = jnp.concatenate(ys, axis=-1) * cs_ref[...]
    o_ref[...] = h + mod_ref[2:3, :] * _rms(y, ng_ref[1:2, :])


def _pool_layer(h, mod, norm_g, layer, seq, w_grp, ch_scale):
    n, d = h.shape
    tm = _pick(seq, 512)
    mod_spec, ng_spec = _cond_specs(layer, tm, seq, d, 1)
    g, gd, _ = w_grp.shape
    kern = functools.partial(_pool_kernel, tm=tm, seq=seq)
    return pl.pallas_call(
        kern,
        out_shape=jax.ShapeDtypeStruct((n, d), F32),
        grid=(n // tm,),
        in_specs=[
            pl.BlockSpec((tm, d), lambda m: (m, 0)),
            mod_spec, ng_spec,
            pl.BlockSpec((g, gd, gd), lambda m: (0, 0, 0)),
            pl.BlockSpec((1, d), lambda m: (0, 0)),
        ],
        out_specs=pl.BlockSpec((tm, d), lambda m: (m, 0)),
        scratch_shapes=[
            pltpu.VMEM((tm + POOL_HALO, d), F32),
            pltpu.VMEM((POOL_HALO, d), F32),
        ],
        compiler_params=_params(("arbitrary",)),
        name="pool_mixer",
    )(h, mod, norm_g, w_grp, ch_scale.reshape(1, d))


CF_HALO = 32
CF_ROWS = 64


def _conformer_kernel(h_ref, mod_ref, ng_ref, wa_ref, wgt_ref, ba_ref, bgt_ref, dw_ref, bdw_ref,
                      lng_ref, lnb_ref, w2_ref, b2_ref, o_ref,
                      u_scr, ext_scr, carry_scr, sh_scr, z_scr, zb_scr, *, tm, seq, width):
    m = pl.program_id(0)
    c = pl.program_id(1)
    nc = pl.num_programs(1)
    tc = ext_scr.shape[-1]

    @pl.when(c == 0)
    def _():
        u = _prenorm(h_ref[...], ng_ref[0:1, :], mod_ref[0:1, :], mod_ref[1:2, :])
        u_scr[...] = u.astype(BF16)

    u = u_scr[...]
    a = _dot(u, wa_ref[...]) + ba_ref[...]
    gate = _dot(u, wgt_ref[...]) + bgt_ref[...]
    glu = a / (1.0 + jnp.exp(-gate))
    _load_halo(ext_scr, carry_scr.at[c], (m * tm) % seq == 0)
    ext_scr[CF_HALO:CF_HALO + tm, :] = glu
    carry_scr[c] = glu[tm - CF_HALO:, :]

    sh_rows = sh_scr.shape[1]
    for b in range(1, SUBLANES):
        sh_scr[b - 1] = ext_scr[pl.ds(b, sh_rows), :]

    base = CF_HALO - (width - 1)

    def rows(r, carry):
        r0 = pl.multiple_of(r * CF_ROWS, CF_ROWS)
        acc = jnp.zeros((CF_ROWS, tc), F32) + bdw_ref[...]
        for k in range(width):
            a8, b = divmod(base + k, SUBLANES)
            src = ext_scr if b == 0 else sh_scr.at[b - 1]
            acc = acc + dw_ref[k:k + 1, :] * src[pl.ds(r0 + a8 * SUBLANES, CF_ROWS), :]
        z_scr[c, pl.ds(r0, CF_ROWS), :] = acc
        return carry

    lax.fori_loop(0, tm // CF_ROWS, rows, 0)

    @pl.when(c == nc - 1)
    def _():
        nblk = z_scr.shape[0]
        d = nblk * tc
        tot = jnp.zeros((tm, 1), F32)
        for cb in range(nblk):
            tot = tot + jnp.sum(z_scr[cb], axis=-1, keepdims=True)
        mu = tot / d
        var = jnp.zeros((tm, 1), F32)
        for cb in range(nblk):
            dv = z_scr[cb] - mu
            var = var + jnp.sum(dv * dv, axis=-1, keepdims=True)
        inv = lax.rsqrt(var / d + LN_EPS)
        for cb in range(nblk):
            lo, hi = cb * tc, (cb + 1) * tc
            zn = (z_scr[cb] - mu) * inv * lng_ref[:, lo:hi] + lnb_ref[:, lo:hi]
            zb_scr[:, lo:hi] = _silu(zn).astype(BF16)
        y = _dot(zb_scr[...], w2_ref[...]) + b2_ref[...]
        o_ref[...] = h_ref[...] + mod_ref[2:3, :] * _rms(y, ng_ref[1:2, :])


def _conformer_layer(h, mod, norm_g, layer, seq, w_pw1, b_pw1, w_dw, b_dw, ln_g, ln_b, w_pw2, b_pw2):
    n, d = h.shape
    tm = _pick(seq, 512)
    tc = _pick(d, 512)
    nc = d // tc
    width = w_dw.shape[0]
    mod_spec, ng_spec = _cond_specs(layer, tm, seq, d, 2)
    kern = functools.partial(_conformer_kernel, tm=tm, seq=seq, width=width)
    row = lambda v: v.reshape(1, -1)
    return pl.pallas_call(
        kern,
        out_shape=jax.ShapeDtypeStruct((n, d), F32),
        grid=(n // tm, nc),
        in_specs=[
            pl.BlockSpec((tm, d), lambda m, c: (m, 0), pipeline_mode=pl.Buffered(1)),
            mod_spec, ng_spec,
            pl.BlockSpec((d, tc), lambda m, c: (0, c)),
            pl.BlockSpec((d, tc), lambda m, c: (0, nc + c)),
            pl.BlockSpec((1, tc), lambda m, c: (0, c)),
            pl.BlockSpec((1, tc), lambda m, c: (0, nc + c)),
            pl.BlockSpec((width, tc), lambda m, c: (0, c)),
            pl.BlockSpec((1, tc), lambda m, c: (0, c)),
            pl.BlockSpec((1, d), lambda m, c: (0, 0)),
            pl.BlockSpec((1, d), lambda m, c: (0, 0)),
            pl.BlockSpec((d, d), lambda m, c: (0, 0), pipeline_mode=pl.Buffered(1)),
            pl.BlockSpec((1, d), lambda m, c: (0, 0)),
        ],
        out_specs=pl.BlockSpec((tm, d), lambda m, c: (m, 0)),
        scratch_shapes=[
            pltpu.VMEM((tm, d), BF16),
            pltpu.VMEM((tm + CF_HALO, tc), F32),
            pltpu.VMEM((nc, CF_HALO, tc), F32),
            pltpu.VMEM((SUBLANES - 1, tm + CF_HALO - SUBLANES, tc), F32),
            pltpu.VMEM((nc, tm, tc), F32),
            pltpu.VMEM((tm, d), BF16),
        ],
        compiler_params=_params(("arbitrary", "arbitrary")),
        name="conformer_mixer",
    )(h, mod, norm_g, w_pw1, w_pw1, row(b_pw1), row(b_pw1), w_dw, row(b_dw), row(ln_g), row(ln_b),
      w_pw2, row(b_pw2))


def _ffn_kernel(h_ref, mod_ref, ng_ref, wg_ref, wu_ref, wd_ref, o_ref, u_scr):
    f = pl.program_id(1)

    @pl.when(f == 0)
    def _():
        u = _prenorm(h_ref[...], ng_ref[2:3, :], mod_ref[3:4, :], mod_ref[4:5, :])
        u_scr[...] = u.astype(BF16)
        o_ref[...] = jnp.zeros_like(o_ref)

    u = u_scr[...]
    a = (_silu(_dot(u, wg_ref[...])) * _dot(u, wu_ref[...])).astype(BF16)
    o_ref[...] += _dot(a, wd_ref[...])

    @pl.when(f == pl.num_programs(1) - 1)
    def _():
        o_ref[...] = h_ref[...] + mod_ref[5:6, :] * _rms(o_ref[...], ng_ref[3:4, :])


def _ffn_layer(h, mod, norm_g, layer, seq, w_gate, w_up, w_down):
    n, d = h.shape
    ff = w_gate.shape[-1]
    tm = _pick(seq, 1024)
    tf = _pick(ff, 512)
    mod_spec, ng_spec = _cond_specs(layer, tm, seq, d, 2)
    return pl.pallas_call(
        _ffn_kernel,
        out_shape=jax.ShapeDtypeStruct((n, d), F32),
        grid=(n // tm, ff // tf),
        in_specs=[
            pl.BlockSpec((tm, d), lambda m, f: (m, 0), pipeline_mode=pl.Buffered(1)),
            mod_spec, ng_spec,
            pl.BlockSpec((d, tf), lambda m, f: (0, f)),
            pl.BlockSpec((d, tf), lambda m, f: (0, f)),
            pl.BlockSpec((tf, d), lambda m, f: (f, 0)),
        ],
        out_specs=pl.BlockSpec((tm, d), lambda m, f: (m, 0)),
        scratch_shapes=[pltpu.VMEM((tm, d), BF16)],
        compiler_params=_params(("arbitrary", "arbitrary")),
        name="dense_ffn",
    )(h, mod, norm_g, w_gate, w_up, w_down)


def _router_kernel(h_ref, mod_ref, ng_ref, wr_ref, up_ref, route_ref, cnt_ref, run_scr):
    m = pl.program_id(0)

    @pl.when(m == 0)
    def _():
        run_scr[...] = jnp.zeros_like(run_scr)

    u = _prenorm(h_ref[...], ng_ref[2:3, :], mod_ref[3:4, :], mod_ref[4:5, :])
    tr, d = u.shape
    half = d // 2
    ub = u.astype(BF16)
    ur = ub.astype(F32)
    bits = pltpu.bitcast(ur, jnp.uint32)
    up_ref[...] = (bits[:, :half] >> 16) | (bits[:, half:] & jnp.uint32(0xFFFF0000))

    wr = wr_ref[...]
    wh = wr.astype(BF16)
    wl = (wr - wh.astype(F32)).astype(BF16)
    ul = (u - ur).astype(BF16)
    logits = _dot(ub, wh) + (_dot(ub, wl) + _dot(ul, wh))
    ne = logits.shape[-1]
    lane = lax.broadcasted_iota(jnp.int32, (tr, ne), 1).astype(F32)
    m1 = jnp.max(logits, axis=-1, keepdims=True)
    i1 = jnp.min(jnp.where(logits == m1, lane, float(ne)), axis=-1, keepdims=True)
    rest = jnp.where(lane == i1, -jnp.inf, logits)
    m2 = jnp.max(rest, axis=-1, keepdims=True)
    i2 = jnp.min(jnp.where(rest == m2, lane, float(ne)), axis=-1, keepdims=True)
    e = jnp.exp(m2 - m1)
    w1 = 1.0 / (1.0 + e)
    w2 = e / (1.0 + e)

    sel1 = lane == i1
    sel2 = lane == i2
    chosen = jnp.where(sel1 | sel2, 1.0, 0.0)
    r_i = lax.broadcasted_iota(jnp.int32, (tr, tr), 0)
    c_i = lax.broadcasted_iota(jnp.int32, (tr, tr), 1)
    below = jnp.where(r_i > c_i, 1.0, 0.0).astype(BF16)
    before = _dot(below, chosen.astype(BF16)) + run_scr[...]
    r1 = jnp.sum(jnp.where(sel1, before, 0.0), axis=-1, keepdims=True)
    r2 = jnp.sum(jnp.where(sel2, before, 0.0), axis=-1, keepdims=True)
    run_scr[...] += jnp.sum(chosen, axis=0, keepdims=True)
    cnt_ref[...] = run_scr[...]

    col = lax.broadcasted_iota(jnp.int32, (tr, ROUTE_COLS), 1)
    vals = (i1, i2, r1, r2, w1, w2)
    route = jnp.zeros((tr, ROUTE_COLS), F32)
    for k, v in enumerate(vals):
        route = jnp.where(col == k, v, route)
    route_ref[...] = route


def _router(h, mod, norm_g, layer, seq, w_router):
    n, d = h.shape
    ne = w_router.shape[-1]
    tr = _pick(seq, 512)
    mod_spec, ng_spec = _cond_specs(layer, tr, seq, d, 1)
    return pl.pallas_call(
        _router_kernel,
        out_shape=(
            jax.ShapeDtypeStruct((n, d // 2), jnp.uint32),
            jax.ShapeDtypeStruct((n, ROUTE_COLS), F32),
            jax.ShapeDtypeStruct((1, ne), F32),
        ),
        grid=(n // tr,),
        in_specs=[
            pl.BlockSpec((tr, d), lambda m: (m, 0)),
            mod_spec, ng_spec,
            pl.BlockSpec((d, ne), lambda m: (0, 0)),
        ],
        out_specs=(
            pl.BlockSpec((tr, d // 2), lambda m: (m, 0)),
            pl.BlockSpec((tr, ROUTE_COLS), lambda m: (m, 0)),
            pl.BlockSpec((1, ne), lambda m: (0, 0)),
        ),
        scratch_shapes=[pltpu.VMEM((1, ne), F32)],
        compiler_params=_params(("arbitrary",)),
        name="moe_router",
    )(h, mod, norm_g, w_router)


def _row_copy(src, src_row, dst, dst_row, sem):
    return pltpu.make_async_copy(src.at[pl.ds(src_row, 1), :], dst.at[pl.ds(dst_row, 1), :], sem)


def _moe_kernel(te_ref, nv_ref, q_hbm, u_hbm, wg_ref, wu_ref, wd_ref, y_hbm,
                q_smem, xbuf, x_scr, acc, sem_q, sem_g, sem_s, *, n_tok, tm):
    j = pl.program_id(0)
    f = pl.program_id(1)
    half = xbuf.shape[-1]
    nv = nv_ref[j]

    @pl.when((j == 0) & (f == 0))
    def _():
        xbuf[...] = jnp.zeros_like(xbuf)

    @pl.when(nv > 0)
    def _():
        @pl.when(f == 0)
        def _():
            cp = pltpu.make_async_copy(q_hbm.at[j], q_smem, sem_q)
            cp.start()
            cp.wait()

            def gather(i, carry):
                _row_copy(u_hbm, q_smem[0, i] >> 1, xbuf, i, sem_g).start()
                return carry

            lax.fori_loop(0, nv, gather, 0)

            def gather_done(i, carry):
                _row_copy(u_hbm, 0, xbuf, i, sem_g).wait()
                return carry

            lax.fori_loop(0, nv, gather_done, 0)
            w = xbuf[...]
            x_scr[:, :half] = pltpu.bitcast(w << 16, F32).astype(BF16)
            x_scr[:, half:] = pltpu.bitcast(w & jnp.uint32(0xFFFF0000), F32).astype(BF16)
            acc[...] = jnp.zeros_like(acc)

        x = x_scr[...]
        a = (_silu(_dot(x, wg_ref[...])) * _dot(x, wu_ref[...])).astype(BF16)
        acc[...] += _dot(a, wd_ref[...])

        @pl.when(f == pl.num_programs(1) - 1)
        def _():
            def scatter(i, carry):
                q = q_smem[0, i]
                _row_copy(acc, i, y_hbm, (q & 1) * n_tok + (q >> 1), sem_s).start()
                return carry

            lax.fori_loop(0, nv, scatter, 0)

            def scatter_done(i, carry):
                _row_copy(acc, i, y_hbm, 0, sem_s).wait()
                return carry

            lax.fori_loop(0, nv, scatter_done, 0)


def _moe_experts(u_packed, q, tile_expert, n_valid, w_gate, w_up, w_down, tm):
    n, half = u_packed.shape
    d = 2 * half
    ne, _, ff = w_gate.shape
    tf = _pick(ff, 512)
    nf = ff // tf
    n_tiles = q.shape[0]
    kern = functools.partial(_moe_kernel, n_tok=n, tm=tm)

    def f_idx(j, f, nv):
        return jnp.where(nv[j] > 0, f, nf - 1)

    grid_spec = pltpu.PrefetchScalarGridSpec(
        num_scalar_prefetch=2,
        grid=(n_tiles, nf),
        in_specs=[
            pl.BlockSpec(memory_space=pl.ANY),
            pl.BlockSpec(memory_space=pl.ANY),
            pl.BlockSpec((None, d, tf), lambda j, f, te, nv: (te[j], 0, f_idx(j, f, nv))),
            pl.BlockSpec((None, d, tf), lambda j, f, te, nv: (te[j], 0, f_idx(j, f, nv))),
            pl.BlockSpec((None, tf, d), lambda j, f, te, nv: (te[j], f_idx(j, f, nv), 0)),
        ],
        out_specs=pl.BlockSpec(memory_space=pl.ANY),
        scratch_shapes=[
            pltpu.SMEM((1, tm), jnp.int32),
            pltpu.VMEM((tm, half), jnp.uint32),
            pltpu.VMEM((tm, d), BF16),
            pltpu.VMEM((tm, d), F32),
            pltpu.SemaphoreType.DMA,
            pltpu.SemaphoreType.DMA,
            pltpu.SemaphoreType.DMA,
        ],
    )
    return pl.pallas_call(
        kern,
        out_shape=jax.ShapeDtypeStruct((2 * n, d), F32),
        grid_spec=grid_spec,
        compiler_params=_params(("arbitrary", "arbitrary")),
        name="moe_experts",
    )(tile_expert, n_valid, q, u_packed, w_gate, w_up, w_down)


def _combine_kernel(ya_ref, yb_ref, route_ref, h_ref, mod_ref, ng_ref, o_ref):
    r = route_ref[...]
    y = r[:, 4:5] * ya_ref[...] + r[:, 5:6] * yb_ref[...]
    o_ref[...] = h_ref[...] + mod_ref[5:6, :] * _rms(y, ng_ref[3:4, :])


def _combine(y2, route, h, mod, norm_g, layer, seq):
    n, d = h.shape
    tm = _pick(seq, 512)
    nb = n // tm
    mod_spec, ng_spec = _cond_specs(layer, tm, seq, d, 1)
    return pl.pallas_call(
        _combine_kernel,
        out_shape=jax.ShapeDtypeStruct((n, d), F32),
        grid=(nb,),
        in_specs=[
            pl.BlockSpec((tm, d), lambda m: (m, 0)),
            pl.BlockSpec((tm, d), lambda m: (m + nb, 0)),
            pl.BlockSpec((tm, ROUTE_COLS), lambda m: (m, 0)),
            pl.BlockSpec((tm, d), lambda m: (m, 0)),
            mod_spec, ng_spec,
        ],
        out_specs=pl.BlockSpec((tm, d), lambda m: (m, 0)),
        compiler_params=_params(("arbitrary",)),
        name="moe_combine",
    )(y2, y2, route, h, mod, norm_g)


def _moe_layer(h, mod, norm_g, layer, seq, w_router, w_gate, w_up, w_down):
    n, d = h.shape
    ne = w_router.shape[-1]
    tm = _pick(seq, 1024)
    u_packed, route, counts = _router(h, mod, norm_g, layer, seq, w_router)

    n_tiles = (2 * n) // tm + ne
    e1 = route[:, 0].astype(jnp.int32)
    e2 = route[:, 1].astype(jnp.int32)
    cnt = counts[0].astype(jnp.int32)
    gsz = ((cnt + tm - 1) // tm) * tm
    ends = jnp.cumsum(gsz)
    offs = ends - gsz
    pos1 = offs[e1] + route[:, 2].astype(jnp.int32)
    pos2 = offs[e2] + route[:, 3].astype(jnp.int32)
    tok2 = 2 * jnp.arange(n, dtype=jnp.int32)
    q = jnp.zeros((n_tiles * tm,), jnp.int32)
    q = q.at[pos1].set(tok2, unique_indices=True).at[pos2].set(tok2 + 1, unique_indices=True)
    n_active = ends[-1] // tm
    tiles = jnp.arange(n_tiles, dtype=jnp.int32)
    te = jnp.searchsorted(ends, jnp.minimum(tiles, n_active - 1) * tm, side="right")
    te = jnp.minimum(te, ne - 1).astype(jnp.int32)
    n_valid = jnp.clip(offs[te] + cnt[te] - tiles * tm, 0, tm).astype(jnp.int32)

    y2 = _moe_experts(u_packed, q.reshape(n_tiles, 1, tm), te, n_valid, w_gate, w_up, w_down, tm)
    return _combine(y2, route, h, mod, norm_g, layer, seq)


def kernel(x, c, ada_w, ada_b, norm_g, sc_w_in, sc_conv, sc_w_out, pool_w_grp, pool_scale, cf_w_pw1,
           cf_b_pw1, cf_w_dw, cf_b_dw, cf_ln_g, cf_ln_b, cf_w_pw2, cf_b_pw2, ffn_w_gate, ffn_w_up,
           ffn_w_down, moe_router, moe_w_gate, moe_w_up, moe_w_down):
    b, seq, d = x.shape
    depth = ada_w.shape[0]
    bf = lambda w: w.astype(BF16)

    mod = _ada(c, ada_w, ada_b).reshape(depth, b, 6, d)
    h = x.reshape(b * seq, d)
    for i in range(depth):
        kind, j = i % N_MIXERS, i // N_MIXERS
        if kind == 0:
            h = _sconv_layer(h, mod, norm_g, i, seq, bf(sc_w_in[j]), sc_conv[j], bf(sc_w_out[j]))
        elif kind == 1:
            h = _pool_layer(h, mod, norm_g, i, seq, bf(pool_w_grp[j]), pool_scale[j])
        else:
            h = _conformer_layer(h, mod, norm_g, i, seq, bf(cf_w_pw1[j]), cf_b_pw1[j], cf_w_dw[j],
                                 cf_b_dw[j], cf_ln_g[j], cf_ln_b[j], bf(cf_w_pw2[j]), cf_b_pw2[j])
        f = i // 2
        if i % 2 == 0:
            h = _ffn_layer(h, mod, norm_g, i, seq, bf(ffn_w_gate[f]), bf(ffn_w_up[f]), bf(ffn_w_down[f]))
        else:
            h = _moe_layer(h, mod, norm_g, i, seq, moe_router[f], bf(moe_w_gate[f]), bf(moe_w_up[f]),
                           bf(moe_w_down[f]))
    return h.reshape(b, seq, d)
```
